```python
import math
import jax, jax.numpy as jnp
from jax import lax
import numpy as np

D_MODEL = 2048
BATCH = 8
SEQ = 2048
DEPTH = 2
DEC_BATCH = 32
DEC_SEQ = 32
PAST_LEN = 4096

CHUNK = 64
Q_BLOCK = 128
D_MIX = D_MODEL
ATT_WIDTH = D_MIX // 2
ML_WIDTH = D_MIX - ATT_WIDTH
ATT_HEADS = 8
ATT_DV = ATT_WIDTH // ATT_HEADS
ATT_DQK = ATT_DV // 2
ML_HEADS = 8
ML_DK = ML_WIDTH // ML_HEADS
ML_DV = ML_WIDTH // ML_HEADS
CONV_W = 4
D_FF = 4 * D_MODEL
EPS = 1e-6
N_IN = 3 * ATT_WIDTH + 4 * ML_WIDTH + 2 * ML_HEADS

kernel_name = 'hymba_diffattn_mlstm_stream_step'


def rms_norm(x, w):
    xf = x.astype(jnp.float32)
    y = xf * lax.rsqrt(jnp.mean(xf * xf, axis=-1, keepdims=True) + EPS)
    return (y * w.astype(jnp.float32)).astype(x.dtype)


def split_in(h):
    sizes = (ATT_WIDTH,) * 3 + (ML_WIDTH,) * 4 + (ML_HEADS,) * 2
    cuts = [int(c) for c in np.cumsum(sizes)[:-1]]
    return jnp.split(h, cuts, axis=-1)


def causal_conv(u, buf, w, b):
    L = u.shape[1]
    full = jnp.concatenate([buf.astype(u.dtype), u], axis=1)
    acc = b
    for j in range(CONV_W):
        acc = acc + full[:, j:j + L] * w[j]
    return jax.nn.silu(acc), full[:, L:]


def diff_attn_block(q, k, v, lam, mask):
    s = jnp.einsum('bqhcd,bkhcd->bhcqk', q.astype(jnp.float32), k.astype(jnp.float32)) * (ATT_DQK ** -0.5)
    if mask is not None:
        s = jnp.where(mask, s, -jnp.inf)
    p = jax.nn.softmax(s, axis=-1)
    a = p[:, :, 0] - lam * p[:, :, 1]
    return jnp.einsum('bhqk,bkhd->bqhd', a, v.astype(jnp.float32)).astype(v.dtype)


def diff_attn_prompt(q, k, v, lam):
    B, L = q.shape[0], q.shape[1]
    nb = L // Q_BLOCK
    qb = jnp.moveaxis(q.reshape(B, nb, Q_BLOCK, ATT_HEADS, 2, ATT_DQK), 1, 0)
    key_chunk = jnp.arange(L) // CHUNK

    def body(args):
        i, qi = args
        q_chunk = (i * Q_BLOCK + jnp.arange(Q_BLOCK)) // CHUNK
        mask = key_chunk[None, :] <= q_chunk[:, None]
        return diff_attn_block(qi, k, v, lam, mask)

    out = lax.map(body, (jnp.arange(nb), qb))
    return jnp.moveaxis(out, 0, 1).reshape(B, L, ATT_HEADS, ATT_DV)


def mlstm_chunk(carry, inp):
    C, n, m = carry
    q, k, v, ig, lf = inp
    L = q.shape[2]
    b = jnp.cumsum(lf, axis=-1)
    causal = jnp.tril(jnp.ones((L, L), dtype=bool))
    log_d = jnp.where(causal, b[..., :, None] - b[..., None, :] + ig[..., None, :], -jnp.inf)
    log_inter = b + m[..., None]
    m_t = jnp.maximum(log_inter, jnp.max(log_d, axis=-1))
    d = jnp.exp(log_d - m_t[..., None])
    w_inter = jnp.exp(log_inter - m_t)
    qk = jnp.einsum('bhtd,bhsd->bhts', q, k) * d
    num = w_inter[..., None] * jnp.einsum('bhtd,bhde->bhte', q, C) + jnp.einsum('bhts,bhse->bhte', qk, v)
    den = w_inter * jnp.einsum('bhtd,bhd->bht', q, n) + jnp.sum(qk, axis=-1)
    h = num / jnp.maximum(jnp.abs(den), jnp.exp(-m_t))[..., None]
    m_new = m_t[..., -1]
    w_c = jnp.exp(b[..., -1] + m - m_new)
    w_s = jnp.exp(b[..., -1:] - b + ig - m_new[..., None])
    C_new = w_c[..., None, None] * C + jnp.einsum('bhs,bhsd,bhse->bhde', w_s, k, v)
    n_new = w_c[..., None] * n + jnp.einsum('bhs,bhsd->bhd', w_s, k)
    return (C_new, n_new, m_new), h


def mlstm_prompt(carry, q, k, v, ig, lf):
    B, H, L, _ = q.shape
    nc = L // CHUNK
    blk = lambda t: jnp.moveaxis(t.reshape(B, H, nc, CHUNK, *t.shape[3:]), 2, 0)
    carry, h = lax.scan(mlstm_chunk, carry, (blk(q), blk(k), blk(v), blk(ig), blk(lf)))
    return carry, jnp.moveaxis(h, 0, 2).reshape(B, H, L, ML_DV)


def heads(u):
    B, L, _ = u.shape
    return u.reshape(B, L, ML_HEADS, -1).transpose(0, 2, 1, 3).astype(jnp.float32)


def layer(x, l, p, past):
    B, L, _ = x.shape
    xn = rms_norm(x, p['norm1_w'])
    aq, ak, av, mq, mk, mv, mo, mi, mf = split_in(xn @ p['w_in'])

    q = rms_norm(aq.reshape(B, L, ATT_HEADS, 2, ATT_DQK), p['q_norm_w'])
    k = rms_norm(ak.reshape(B, L, ATT_HEADS, 2, ATT_DQK), p['k_norm_w'])
    v = av.reshape(B, L, ATT_HEADS, ATT_DV)
    lam_init = 0.8 - 0.6 * math.exp(-0.3 * l)
    lqk = p['lambda_qk'].astype(jnp.float32)
    lam = jnp.exp(jnp.sum(lqk[0] * lqk[1])) - jnp.exp(jnp.sum(lqk[2] * lqk[3])) + lam_init

    if past is None:
        att = diff_attn_prompt(q, k, v, lam)
        conv_buf = jnp.zeros((B, CONV_W - 1, 2 * ML_WIDTH), x.dtype)
        C0 = jnp.zeros((B, ML_HEADS, ML_DK, ML_DV), jnp.float32)
        n0 = jnp.zeros((B, ML_HEADS, ML_DK), jnp.float32)
        m0 = jnp.zeros((B, ML_HEADS), jnp.float32)
    else:
        k_cache, v_cache, C0, n0, m0, conv_buf = past
        k_all = jnp.concatenate([k_cache.reshape(B, -1, ATT_HEADS, 2, ATT_DQK).astype(k.dtype), k], axis=1)
        v_all = jnp.concatenate([v_cache.astype(v.dtype), v], axis=1)
        att = diff_attn_block(q, k_all, v_all, lam, None)
        C0 = C0.astype(jnp.float32)
        n0 = n0.astype(jnp.float32)
        m0 = m0.astype(jnp.float32)
    att = (rms_norm(att, p['att_norm_w']) * (1.0 - lam_init)).reshape(B, L, ATT_WIDTH).astype(x.dtype)

    qk_c, conv_new = causal_conv(jnp.concatenate([mq, mk], axis=-1), conv_buf, p['conv_w'], p['conv_b'])
    qh = heads(qk_c[..., :ML_WIDTH])
    kh = heads(qk_c[..., ML_WIDTH:]) * (ML_DK ** -0.5)
    vh = heads(mv)
    gates = (jnp.concatenate([mi, mf], axis=-1) + p['b_gates']).astype(jnp.float32)
    ig = gates[..., :ML_HEADS].transpose(0, 2, 1)
    lf = jax.nn.log_sigmoid(gates[..., ML_HEADS:]).transpose(0, 2, 1)
    if past is None:
        (C1, n1, m1), h = mlstm_prompt((C0, n0, m0), qh, kh, vh, ig, lf)
    else:
        (C1, n1, m1), h = mlstm_chunk((C0, n0, m0), (qh, kh, vh, ig, lf))
    h = h.transpose(0, 2, 1, 3)
    o = jax.nn.sigmoid(mo.astype(jnp.float32)).reshape(B, L, ML_HEADS, ML_DV)
    ml = (rms_norm(h, p['ml_norm_w']) * o).reshape(B, L, ML_WIDTH).astype(x.dtype)

    x = x + jnp.concatenate([att, ml], axis=-1) @ p['w_out']
    hn = rms_norm(x, p['norm2_w'])
    x = x + jnp.square(jax.nn.relu(hn @ p['w_up'])) @ p['w_down']
    return x, (k.reshape(B, L, ATT_HEADS, 2 * ATT_DQK), v, C1, n1, m1, conv_new)


def setup_inputs(seed: int = 0) -> dict:
    key = jax.random.key(seed)
    ks = jax.random.split(key, 24)
    nrm = lambda kk, shape, s: s * jax.random.normal(kk, shape, jnp.float32)
    b_gates = jnp.concatenate([nrm(ks[12], (DEPTH, ML_HEADS), 0.1),
                               3.0 + nrm(ks[13], (DEPTH, ML_HEADS), 0.1)], axis=-1)
    return {
        'x_prompt': nrm(ks[0], (BATCH, SEQ, D_MODEL), 1.0),
        'x_sample': nrm(ks[1], (DEC_BATCH, DEC_SEQ, D_MODEL), 1.0),
        'cache_k': nrm(ks[2], (DEPTH, DEC_BATCH, PAST_LEN, ATT_HEADS, 2 * ATT_DQK), 1.0),
        'cache_v': nrm(ks[3], (DEPTH, DEC_BATCH, PAST_LEN, ATT_HEADS, ATT_DV), 1.0),
        'state_C': nrm(ks[4], (DEPTH, DEC_BATCH, ML_HEADS, ML_DK, ML_DV), 0.1),
        'state_n': nrm(ks[5], (DEPTH, DEC_BATCH, ML_HEADS, ML_DK), 0.1),
        'state_m': nrm(ks[6], (DEPTH, DEC_BATCH, ML_HEADS), 0.5),
        'state_conv': nrm(ks[7], (DEPTH, DEC_BATCH, CONV_W - 1, 2 * ML_WIDTH), 1.0),
        'norm1_w': 1.0 + nrm(ks[8], (DEPTH, D_MODEL), 0.02),
        'w_in': nrm(ks[9], (DEPTH, D_MODEL, N_IN), D_MODEL ** -0.5),
        'conv_w': nrm(ks[10], (DEPTH, CONV_W, 2 * ML_WIDTH), CONV_W ** -0.5),
        'conv_b': nrm(ks[11], (DEPTH, 2 * ML_WIDTH), 0.02),
        'b_gates': b_gates,
        'q_norm_w': 1.0 + nrm(ks[14], (DEPTH, ATT_DQK), 0.02),
        'k_norm_w': 1.0 + nrm(ks[15], (DEPTH, ATT_DQK), 0.02),
        'lambda_qk': nrm(ks[16], (DEPTH, 4, ATT_DQK), 0.1),
        'att_norm_w': 1.0 + nrm(ks[17], (DEPTH, ATT_DV), 0.02),
        'ml_norm_w': 1.0 + nrm(ks[18], (DEPTH, ML_DV), 0.02),
        'w_out': nrm(ks[19], (DEPTH, D_MIX, D_MODEL), D_MIX ** -0.5),
        'norm2_w': 1.0 + nrm(ks[20], (DEPTH, D_MODEL), 0.02),
        'w_up': nrm(ks[21], (DEPTH, D_MODEL, D_FF), D_MODEL ** -0.5),
        'w_down': nrm(ks[22], (DEPTH, D_FF, D_MODEL), D_FF ** -0.5),
    }


def reference(x_prompt, x_sample, cache_k, cache_v, state_C, state_n, state_m, state_conv,
              norm1_w, w_in, conv_w, conv_b, b_gates, q_norm_w, k_norm_w, lambda_qk,
              att_norm_w, ml_norm_w, w_out, norm2_w, w_up, w_down):
    yp, ys = x_prompt, x_sample
    sp, ss = [], []
    for l in range(DEPTH):
        p = {'norm1_w': norm1_w[l], 'w_in': w_in[l], 'conv_w': conv_w[l], 'conv_b': conv_b[l],
             'b_gates': b_gates[l], 'q_norm_w': q_norm_w[l], 'k_norm_w': k_norm_w[l],
             'lambda_qk': lambda_qk[l], 'att_norm_w': att_norm_w[l], 'ml_norm_w': ml_norm_w[l],
             'w_out': w_out[l], 'norm2_w': norm2_w[l], 'w_up': w_up[l], 'w_down': w_down[l]}
        yp, st_p = layer(yp, l, p, None)
        ys, st_s = layer(ys, l, p, (cache_k[l], cache_v[l], state_C[l], state_n[l],
                                    state_m[l], state_conv[l]))
        sp.append(st_p)
        ss.append(st_s)
    k_prompt, v_prompt, C_prompt, n_prompt, m_prompt, conv_prompt = [jnp.stack([s[i] for s in sp]) for i in range(6)]
    k_sample, v_sample, C_sample, n_sample, m_sample, conv_sample = [jnp.stack([s[i] for s in ss]) for i in range(6)]
    return (yp, ys, k_prompt, v_prompt, C_prompt, n_prompt, m_prompt, conv_prompt,
            k_sample, v_sample, C_sample, n_sample, m_sample, conv_sample)
```

```python
import functools
import math

import jax
import jax.numpy as jnp
from jax import lax
from jax.experimental import pallas as pl
from jax.experimental.pallas import tpu as pltpu

F32 = jnp.float32
BF16 = jnp.bfloat16

D_MODEL = 2048
N_HEADS = 8
HEAD_W = 128
DQK = 64
GROUP_W = N_HEADS * HEAD_W
N_MAIN = 7 * GROUP_W
CONV_W = 4
D_FF = 4 * D_MODEL
EPS = 1e-6
PROMPT_CHUNK = 64
NEG = -1e30

LANES = 128
SUBLANES = 8
MIB = 1024 * 1024


def _cparams(semantics, vmem_mib):
    return pltpu.CompilerParams(dimension_semantics=semantics, vmem_limit_bytes=vmem_mib * MIB)


def _dot(a, b):
    return jnp.dot(a, b, preferred_element_type=F32)


def _dot_nt(a, b):
    return lax.dot_general(a, b, (((1,), (1,)), ((), ())), preferred_element_type=F32)


def _dot_tn(a, b):
    return lax.dot_general(a, b, (((0,), (0,)), ((), ())), preferred_element_type=F32)


def _rms_rows(x, w):
    ms = jnp.mean(x * x, axis=-1, keepdims=True)
    return x * lax.rsqrt(ms + EPS) * w


_NORM_ROWS = 128


def _inproj_kernel(x_ref, nw_ref, w_ref, wg_ref, h_ref, g_ref, xn_ref):
    tm = x_ref.shape[0]

    @pl.when(pl.program_id(1) == 0)
    def _():
        def norm_rows(r, carry):
            rows = pl.ds(pl.multiple_of(r * _NORM_ROWS, _NORM_ROWS), _NORM_ROWS)
            xn_ref[rows, :] = _rms_rows(x_ref[rows, :], nw_ref[...]).astype(BF16)
            return carry

        lax.fori_loop(0, tm // _NORM_ROWS, norm_rows, 0)
        g_ref[...] = _dot(xn_ref[...], wg_ref[...])

    h_ref[...] = _dot(xn_ref[...], w_ref[...])


def _inproj(x, norm_w, w_main, w_gate, *, tm, tn):
    t = x.shape[0]
    return pl.pallas_call(
        _inproj_kernel,
        grid=(t // tm, N_MAIN // tn),
        in_specs=[
            pl.BlockSpec((tm, D_MODEL), lambda i, j: (i, 0)),
            pl.BlockSpec((1, D_MODEL), lambda i, j: (0, 0)),
            pl.BlockSpec((D_MODEL, tn), lambda i, j: (0, j)),
            pl.BlockSpec((D_MODEL, LANES), lambda i, j: (0, 0)),
        ],
        out_specs=[
            pl.BlockSpec((tm, tn), lambda i, j: (i, j)),
            pl.BlockSpec((tm, LANES), lambda i, j: (i, 0)),
        ],
        out_shape=[
            jax.ShapeDtypeStruct((t, N_MAIN), F32),
            jax.ShapeDtypeStruct((t, LANES), F32),
        ],
        scratch_shapes=[pltpu.VMEM((tm, D_MODEL), BF16)],
        compiler_params=_cparams(("parallel", "arbitrary"), 48),
        name="inproj",
    )(x, norm_w, w_main, w_gate)


_PREP_ROWS = 64
_PREP_SLAB = 256


def _group_mean_sq(x, gmat):
    sq = x * x
    hi = sq.astype(BF16)
    lo = (sq - hi.astype(F32)).astype(BF16)
    return _dot(hi, gmat) + _dot(lo, gmat)


def _attn_prep_kernel(q_ref, k_ref, v_ref, qw_ref, kw_ref, gmat_ref,
                      qn_ref, ko_ref, kb_ref, vo_ref, vb_ref):
    tm = q_ref.shape[0]
    gmat = gmat_ref[...]

    def rows_body(r, carry):
        start = pl.multiple_of(r * _PREP_ROWS, _PREP_ROWS)
        rows = pl.ds(start, _PREP_ROWS)
        for s in range(GROUP_W // _PREP_SLAB):
            cols = slice(s * _PREP_SLAB, (s + 1) * _PREP_SLAB)
            q = q_ref[rows, cols]
            qn = q * lax.rsqrt(_group_mean_sq(q, gmat) + EPS) * qw_ref[...]
            qn_ref[rows, cols] = (qn * (DQK ** -0.5)).astype(BF16)
            k = k_ref[rows, cols]
            kn = k * lax.rsqrt(_group_mean_sq(k, gmat) + EPS) * kw_ref[...]
            kb_ref[rows, cols] = kn.astype(BF16)
            v = v_ref[rows, cols]
            vb_ref[rows, cols] = v.astype(BF16)
            for hh in range(_PREP_SLAB // HEAD_W):
                head = s * (_PREP_SLAB // HEAD_W) + hh
                dst = pl.ds(start * N_HEADS + head, _PREP_ROWS, stride=N_HEADS)
                ko_ref[dst, :] = kn[:, hh * HEAD_W:(hh + 1) * HEAD_W]
                vo_ref[dst, :] = v[:, hh * HEAD_W:(hh + 1) * HEAD_W]
        return carry

    lax.fori_loop(0, tm // _PREP_ROWS, rows_body, 0)


def _attn_prep(h, qw, kw, gmat, *, tm):
    t = h.shape[0]
    col = lambda c: pl.BlockSpec((tm, GROUP_W), lambda i, c=c: (i, c))
    small = lambda a: pl.BlockSpec(a.shape, lambda i: (0, 0))
    out = pl.BlockSpec((tm, GROUP_W), lambda i: (i, 0))
    state = pl.BlockSpec((tm * N_HEADS, HEAD_W), lambda i: (i, 0))
    return pl.pallas_call(
        _attn_prep_kernel,
        grid=(t // tm,),
        in_specs=[col(0), col(1), col(2), small(qw), small(kw), small(gmat)],
        out_specs=[out, state, out, state, out],
        out_shape=[
            jax.ShapeDtypeStruct((t, GROUP_W), BF16),
            jax.ShapeDtypeStruct((t * N_HEADS, HEAD_W), F32),
            jax.ShapeDtypeStruct((t, GROUP_W), BF16),
            jax.ShapeDtypeStruct((t * N_HEADS, HEAD_W), F32),
            jax.ShapeDtypeStruct((t, GROUP_W), BF16),
        ],
        compiler_params=_cparams(("parallel",), 48),
        name="attn_prep",
    )(h, h, h, qw, kw, gmat)


def _lambda_full(lq_ref, lam_init):
    lq = lq_ref[...]
    s1 = jnp.sum(lq[0:1] * lq[1:2], axis=-1, keepdims=True)
    s2 = jnp.sum(lq[2:3] * lq[3:4], axis=-1, keepdims=True)
    return jnp.exp(s1) - jnp.exp(s2) + lam_init


def _split_halves(q):
    lane = lax.broadcasted_iota(jnp.int32, q.shape, 1)
    zero = jnp.zeros_like(q)
    return jnp.where(lane < DQK, q, zero), jnp.where(lane >= DQK, q, zero)


def _softmax_step(state, s, v):
    m, l, acc = state
    m_new = jnp.maximum(m, jnp.max(s, axis=-1, keepdims=True))
    alpha = jnp.exp(m - m_new)
    p = jnp.exp(s - m_new)
    l = alpha * l + jnp.sum(p, axis=-1, keepdims=True)
    acc = alpha * acc + _dot(p.astype(BF16), v)
    return m_new, l, acc


def _diff_finish(st1, st2, lam, nw, lam_init):
    a = st1[2] / st1[1] - lam * (st2[2] / st2[1])
    return _rms_rows(a, nw) * (1.0 - lam_init)


def _flash_prompt_kernel(lq_ref, nw_ref, q_ref, k_ref, v_ref, o_ref, *, tq, tk, lam_init):
    qi = pl.program_id(2)
    qz = _split_halves(q_ref[...])

    def kv_step(j, states, masked):
        rows = pl.ds(pl.multiple_of(j * tk, tk), tk)
        k = k_ref[rows, :]
        v = v_ref[rows, :]
        new = []
        for c in range(2):
            s = _dot_nt(qz[c], k)
            if masked:
                qpos = qi * tq + lax.broadcasted_iota(jnp.int32, (tq, tk), 0)
                kpos = j * tk + lax.broadcasted_iota(jnp.int32, (tq, tk), 1)
                s = jnp.where(kpos // PROMPT_CHUNK <= qpos // PROMPT_CHUNK, s, NEG)
            new.append(_softmax_step(states[c], s, v))
        return tuple(new)

    init = tuple((jnp.full((tq, 1), NEG, F32), jnp.zeros((tq, 1), F32),
                  jnp.zeros((tq, HEAD_W), F32)) for _ in range(2))
    n_full = qi * (tq // tk)
    states = lax.fori_loop(0, n_full, lambda j, st: kv_step(j, st, False), init)
    for d in range(tq // tk):
        states = kv_step(n_full + d, states, True)
    lam = _lambda_full(lq_ref, lam_init)
    o_ref[...] = _diff_finish(states[0], states[1], lam, nw_ref[...], lam_init).astype(o_ref.dtype)


def _flash_prompt(qn, kb, vb, lam_qk, att_nw, *, batch, seq, tq, tk, lam_init):
    nq = seq // tq
    kernel = functools.partial(_flash_prompt_kernel, tq=tq, tk=tk, lam_init=lam_init)
    kv_spec = pl.BlockSpec((seq, HEAD_W), lambda b, h, i: (b, h))
    return pl.pallas_call(
        kernel,
        grid=(batch, N_HEADS, nq),
        in_specs=[
            pl.BlockSpec(lam_qk.shape, lambda b, h, i: (0, 0)),
            pl.BlockSpec(att_nw.shape, lambda b, h, i: (0, 0)),
            pl.BlockSpec((tq, HEAD_W), lambda b, h, i: (b * nq + i, h)),
            kv_spec, kv_spec,
        ],
        out_specs=pl.BlockSpec((tq, HEAD_W), lambda b, h, i: (b * nq + i, h)),
        out_shape=jax.ShapeDtypeStruct((batch * seq, GROUP_W), BF16),
        compiler_params=_cparams(("parallel", "parallel", "arbitrary"), 32),
        name="diff_attn_prompt",
    )(lam_qk, att_nw, qn, kb, vb)


def _attn_sample_kernel(lq_ref, nw_ref, q_ref, kn_ref, vn_ref, kc_ref, vc_ref, o_ref,
                        m_ref, l_ref, acc_ref, *, lam_init):
    j = pl.program_id(1)

    @pl.when(j == 0)
    def _():
        m_ref[...] = jnp.full(m_ref.shape, NEG, F32)
        l_ref[...] = jnp.zeros(l_ref.shape, F32)
        acc_ref[...] = jnp.zeros(acc_ref.shape, F32)

    def update(h, k, v):
        qz = _split_halves(q_ref[:, h * HEAD_W:(h + 1) * HEAD_W])
        for c in range(2):
            idx = 2 * h + c
            st = _softmax_step((m_ref[idx], l_ref[idx], acc_ref[idx]), _dot_nt(qz[c], k), v)
            m_ref[idx], l_ref[idx], acc_ref[idx] = st

    tk = kc_ref.shape[0] // N_HEADS
    for h in range(N_HEADS):
        head_rows = pl.ds(h, tk, stride=N_HEADS)
        update(h, kc_ref[head_rows, :].astype(BF16), vc_ref[head_rows, :].astype(BF16))

    @pl.when(j == pl.num_programs(1) - 1)
    def _():
        lam = _lambda_full(lq_ref, lam_init)
        for h in range(N_HEADS):
            cols = slice(h * HEAD_W, (h + 1) * HEAD_W)
            update(h, kn_ref[:, cols], vn_ref[:, cols])
            st1 = (m_ref[2 * h], l_ref[2 * h], acc_ref[2 * h])
            st2 = (m_ref[2 * h + 1], l_ref[2 * h + 1], acc_ref[2 * h + 1])
            o_ref[:, cols] = _diff_finish(st1, st2, lam, nw_ref[...], lam_init).astype(o_ref.dtype)


def _attn_sample(qn, kb, vb, cache_k, cache_v, lam_qk, att_nw, *, layer, batch, past, seq, tk,
                 lam_init):
    kernel = functools.partial(_attn_sample_kernel, lam_init=lam_init)
    new_spec = pl.BlockSpec((seq, GROUP_W), lambda b, j: (b, 0))
    nkv = past // tk
    cache_spec = pl.BlockSpec((tk * N_HEADS, HEAD_W), lambda b, j: ((layer * batch + b) * nkv + j, 0))
    return pl.pallas_call(
        kernel,
        grid=(batch, nkv),
        in_specs=[
            pl.BlockSpec(lam_qk.shape, lambda b, j: (0, 0)),
            pl.BlockSpec(att_nw.shape, lambda b, j: (0, 0)),
            new_spec, new_spec, new_spec, cache_spec, cache_spec,
        ],
        out_specs=new_spec,
        out_shape=jax.ShapeDtypeStruct((batch * seq, GROUP_W), BF16),
        scratch_shapes=[
            pltpu.VMEM((2 * N_HEADS, seq, 1), F32),
            pltpu.VMEM((2 * N_HEADS, seq, 1), F32),
            pltpu.VMEM((2 * N_HEADS, seq, HEAD_W), F32),
        ],
        compiler_params=_cparams(("parallel", "arbitrary"), 40),
        name="diff_attn_sample",
    )(lam_qk, att_nw, qn, kb, vb, cache_k, cache_v)


def _log_sigmoid(x):
    return jnp.minimum(x, 0.0) - jnp.log(1.0 + jnp.exp(-jnp.abs(x)))


def _cumsum(x, n, axis):
    pos = lax.broadcasted_iota(jnp.int32, x.shape, axis)
    s = 1
    while s < n:
        x = x + jnp.where(pos >= s, pltpu.roll(x, s, axis), 0.0)
        s *= 2
    return x


def _causal_conv_silu(win, w_ref, b_ref, cols, ch):
    acc = b_ref[:, cols] + w_ref[CONV_W - 1:CONV_W, cols] * win[SUBLANES:, :]
    for j in range(CONV_W - 1):
        shifted = pltpu.roll(win, CONV_W - 1 - j, 0)[SUBLANES:, :]
        acc = acc + w_ref[j:j + 1, cols] * shifted
    return acc * jax.nn.sigmoid(acc)


def _mlstm_kernel(xq_ref, xk_ref, xv_ref, xo_ref, hq_ref, hk_ref, pad_ref, gcol_ref, grow_ref,
                  bcol_ref, brow_ref, cw_ref, cb_ref, nw_ref, c0_ref, n0_ref, m0_ref,
                  o_ref, c1_ref, n1_ref, m1_ref,
                  qext_ref, kext_ref, c_ref, n_ref, m_ref, *, ch):
    r = pl.program_id(1)
    rb = xq_ref.shape[0]
    first = r == 0

    @pl.when(first)
    def _():
        c_ref[...] = c0_ref[0]
        n_ref[...] = n0_ref[0]
        m_ref[...] = m0_ref[0]

    qext_ref[0:SUBLANES, :] = jnp.where(first, pad_ref[0, :, 0:GROUP_W], hq_ref[...])
    kext_ref[0:SUBLANES, :] = jnp.where(first, pad_ref[0, :, GROUP_W:2 * GROUP_W], hk_ref[...])
    qext_ref[SUBLANES:, :] = xq_ref[...]
    kext_ref[SUBLANES:, :] = xk_ref[...]

    row_i = lax.broadcasted_iota(jnp.int32, (ch, ch), 0)
    col_i = lax.broadcasted_iota(jnp.int32, (ch, ch), 1)
    causal = col_i <= row_i

    def chunk(c, carry):
        start = pl.multiple_of(c * ch, ch)
        rows = pl.ds(start, ch)
        win = pl.ds(start, ch + SUBLANES)
        qc = _causal_conv_silu(qext_ref[win, :], cw_ref, cb_ref, slice(0, GROUP_W), ch).astype(BF16)
        kc = _causal_conv_silu(kext_ref[win, :], cw_ref, cb_ref, slice(GROUP_W, 2 * GROUP_W), ch)
        kc = (kc * (HEAD_W ** -0.5)).astype(BF16)

        g = gcol_ref[rows, :] + bcol_ref[...]
        b_col = _cumsum(_log_sigmoid(g), ch, 0)
        gr = grow_ref[0, c] + brow_ref[...]
        b_row = _cumsum(_log_sigmoid(gr[N_HEADS:2 * N_HEADS]), ch, 1)
        r_row = gr[0:N_HEADS] - b_row

        for h in range(N_HEADS):
            cols = slice(h * HEAD_W, (h + 1) * HEAD_W)
            q = qc[:, cols]
            k = kc[:, cols]
            v = xv_ref[rows, cols].astype(BF16)
            bc = b_col[:, N_HEADS + h:N_HEADS + h + 1]
            igc = g[:, h:h + 1]
            m_prev = m_ref[h:h + 1, 0:1]
            n_prev = n_ref[h:h + 1, :]
            c_prev = c_ref[h]

            log_d = jnp.where(causal, bc + r_row[h:h + 1, 0:ch], NEG)
            log_inter = bc + m_prev
            m_t = jnp.maximum(log_inter, jnp.max(log_d, axis=-1, keepdims=True))
            d = jnp.exp(log_d - m_t)
            w_inter = jnp.exp(log_inter - m_t)
            qk = _dot_nt(q, k) * d
            num = w_inter * _dot(q, c_prev.astype(BF16)) + _dot(qk.astype(BF16), v)
            qn = jnp.sum(q.astype(F32) * n_prev, axis=-1, keepdims=True)
            den = w_inter * qn + jnp.sum(qk, axis=-1, keepdims=True)
            hv = num / jnp.maximum(jnp.abs(den), jnp.exp(-m_t))

            m_new = m_t[ch - 1:ch, :]
            b_last = bc[ch - 1:ch, :]
            w_c = jnp.exp(b_last + m_prev - m_new)
            w_s = jnp.exp(b_last - bc + igc - m_new)
            kw = k.astype(F32) * w_s
            c_ref[h] = w_c * c_prev + _dot_tn(kw.astype(BF16), v)
            n_ref[h:h + 1, :] = w_c * n_prev + jnp.sum(kw, axis=0, keepdims=True)
            m_ref[h:h + 1, :] = jnp.broadcast_to(m_new, (1, LANES))

            gate = jax.nn.sigmoid(xo_ref[rows, cols])
            o_ref[rows, cols] = (_rms_rows(hv, nw_ref[...]) * gate).astype(o_ref.dtype)
        return carry

    lax.fori_loop(0, rb // ch, chunk, 0)

    @pl.when(r == pl.num_programs(1) - 1)
    def _():
        c1_ref[0] = c_ref[...]
        n1_ref[0] = n_ref[...]
        m1_ref[0] = m_ref[...]


def _mlstm(h, gcol, grow, bcol, brow, conv_pad, conv_w, conv_b, ml_nw, c0, n0, m0,
           *, batch, seq, rb, ch):
    nblk = seq // rb
    kernel = functools.partial(_mlstm_kernel, ch=ch)
    col = lambda c: pl.BlockSpec((rb, GROUP_W), lambda b, r, c=c: (b * nblk + r, c))
    halo = lambda c: pl.BlockSpec(
        (SUBLANES, GROUP_W),
        lambda b, r, c=c: (jnp.maximum((b * seq + r * rb) // SUBLANES - 1, 0), c))
    whole = lambda a: pl.BlockSpec(a.shape, lambda b, r: (0,) * a.ndim)
    per_b = lambda a: pl.BlockSpec((1,) + a.shape[1:], lambda b, r: (b,) + (0,) * (a.ndim - 1))
    return pl.pallas_call(
        kernel,
        grid=(batch, nblk),
        in_specs=[
            col(3), col(4), col(5), col(6), halo(3), halo(4), per_b(conv_pad),
            pl.BlockSpec((rb, LANES), lambda b, r: (b * nblk + r, 0)),
            pl.BlockSpec((1, rb // ch, 2 * N_HEADS, LANES), lambda b, r: (b, r, 0, 0)),
            whole(bcol), whole(brow), whole(conv_w), whole(conv_b), whole(ml_nw),
            per_b(c0), per_b(n0), per_b(m0),
        ],
        out_specs=[
            pl.BlockSpec((rb, GROUP_W), lambda b, r: (b * nblk + r, 0)),
            per_b(c0), per_b(n0), per_b(m0),
        ],
        out_shape=[
            jax.ShapeDtypeStruct((batch * seq, GROUP_W), BF16),
            jax.ShapeDtypeStruct(c0.shape, F32),
            jax.ShapeDtypeStruct(n0.shape, F32),
            jax.ShapeDtypeStruct(m0.shape, F32),
        ],
        scratch_shapes=[
            pltpu.VMEM((rb + SUBLANES, GROUP_W), F32),
            pltpu.VMEM((rb + SUBLANES, GROUP_W), F32),
            pltpu.VMEM((N_HEADS, HEAD_W, HEAD_W), F32),
            pltpu.VMEM((N_HEADS, HEAD_W), F32),
            pltpu.VMEM((N_HEADS, LANES), F32),
        ],
        compiler_params=_cparams(("parallel", "arbitrary"), 48),
        name="mlstm",
    )(h, h, h, h, h, h, conv_pad, gcol, grow, bcol, brow, conv_w, conv_b, ml_nw, c0, n0, m0)


def _outproj_kernel(x_ref, a_ref, m_ref, wa_ref, wm_ref, o_ref):
    o_ref[...] = x_ref[...] + _dot(a_ref[...], wa_ref[...]) + _dot(m_ref[...], wm_ref[...])


def _outproj(x, att, ml, w_out, *, tm, tn):
    t = x.shape[0]
    return pl.pallas_call(
        _outproj_kernel,
        grid=(t // tm, D_MODEL // tn),
        in_specs=[
            pl.BlockSpec((tm, tn), lambda i, j: (i, j)),
            pl.BlockSpec((tm, GROUP_W), lambda i, j: (i, 0)),
            pl.BlockSpec((tm, GROUP_W), lambda i, j: (i, 0)),
            pl.BlockSpec((GROUP_W, tn), lambda i, j: (0, j)),
            pl.BlockSpec((GROUP_W, tn), lambda i, j: (1, j)),
        ],
        out_specs=pl.BlockSpec((tm, tn), lambda i, j: (i, j)),
        out_shape=jax.ShapeDtypeStruct((t, D_MODEL), F32),
        compiler_params=_cparams(("parallel", "parallel"), 48),
        name="outproj",
    )(x, att, ml, w_out, w_out)


def _mlp_kernel(x_ref, nw_ref, wu_ref, wd_ref, o_ref, xn_ref):
    tm = x_ref.shape[0]
    f = pl.program_id(1)

    @pl.when(f == 0)
    def _():
        def norm_rows(r, carry):
            rows = pl.ds(pl.multiple_of(r * _NORM_ROWS, _NORM_ROWS), _NORM_ROWS)
            xn_ref[rows, :] = _rms_rows(x_ref[rows, :], nw_ref[...]).astype(BF16)
            return carry

        lax.fori_loop(0, tm // _NORM_ROWS, norm_rows, 0)

    up = jnp.maximum(_dot(xn_ref[...], wu_ref[...]), 0.0)
    part = _dot((up * up).astype(BF16), wd_ref[...])

    @pl.when(f == 0)
    def _():
        o_ref[...] = x_ref[...] + part

    @pl.when(f != 0)
    def _():
        o_ref[...] += part


def _mlp(x, norm_w, w_up, w_down, *, tm, tf):
    t = x.shape[0]
    return pl.pallas_call(
        _mlp_kernel,
        grid=(t // tm, D_FF // tf),
        in_specs=[
            pl.BlockSpec((tm, D_MODEL), lambda i, f: (i, 0)),
            pl.BlockSpec((1, D_MODEL), lambda i, f: (0, 0)),
            pl.BlockSpec((D_MODEL, tf), lambda i, f: (0, f)),
            pl.BlockSpec((tf, D_MODEL), lambda i, f: (f, 0)),
        ],
        out_specs=pl.BlockSpec((tm, D_MODEL), lambda i, f: (i, 0)),
        out_shape=jax.ShapeDtypeStruct((t, D_MODEL), F32),
        scratch_shapes=[pltpu.VMEM((tm, D_MODEL), BF16)],
        compiler_params=_cparams(("parallel", "arbitrary"), 52),
        name="mlp",
    )(x, norm_w, w_up, w_down)


def _group_matrix():
    g = jnp.arange(_PREP_SLAB) // DQK
    return jnp.where(g[:, None] == g[None, :], 1.0 / DQK, 0.0).astype(BF16)


def _layer(x, l, p, past, *, batch, seq, cfg):
    lam_init = 0.8 - 0.6 * math.exp(-0.3 * l)
    ch = min(seq, PROMPT_CHUNK)
    row2 = lambda a: a.reshape(1, -1)

    h, gates = _inproj(x, row2(p['norm1_w']), p['w_main'], p['w_gate'], tm=cfg['tm_in'], tn=cfg['tn_in'])

    qw = row2(jnp.tile(p['q_norm_w'], _PREP_SLAB // DQK))
    kw = row2(jnp.tile(p['k_norm_w'], _PREP_SLAB // DQK))
    qn, k_out, kb, v_out, vb = _attn_prep(h, qw, kw, _group_matrix(), tm=cfg['tm_prep'])

    att_nw = row2(p['att_norm_w'])
    if past is None:
        att = _flash_prompt(qn, kb, vb, p['lambda_qk'], att_nw, batch=batch, seq=seq,
                            tq=cfg['tq'], tk=cfg['tk'], lam_init=lam_init)
        conv_buf = jnp.zeros((batch, CONV_W - 1, 2 * GROUP_W), F32)
        c0 = jnp.zeros((batch, N_HEADS, HEAD_W, HEAD_W), F32)
        n0 = jnp.zeros((batch, N_HEADS, HEAD_W), F32)
        m0 = jnp.zeros((batch, N_HEADS), F32)
    else:
        cache_k, cache_v, c0, n0, m0, conv_buf = past
        att = _attn_sample(qn, kb, vb, cache_k, cache_v, p['lambda_qk'], att_nw, layer=l,
                           batch=batch, past=cfg['past'], seq=seq, tk=cfg['tk_cache'],
                           lam_init=lam_init)

    n_chunks = seq // ch
    g16 = gates[:, :2 * N_HEADS].reshape(batch, n_chunks, ch, 2 * N_HEADS)
    grow = jnp.pad(jnp.swapaxes(g16, 2, 3), ((0, 0), (0, 0), (0, 0), (0, LANES - ch)))
    bcol = jnp.pad(p['b_gates'], (0, LANES - 2 * N_HEADS)).reshape(1, LANES)
    brow = p['b_gates'].reshape(2 * N_HEADS, 1)
    conv_pad = jnp.pad(conv_buf, ((0, 0), (SUBLANES - (CONV_W - 1), 0), (0, 0)))
    m0_rep = jnp.broadcast_to(m0[:, :, None], (batch, N_HEADS, LANES))
    ml, c1, n1, m1 = _mlstm(h, gates, grow, bcol, brow, conv_pad, p['conv_w'], row2(p['conv_b']),
                            row2(p['ml_norm_w']), c0, n0, m0_rep,
                            batch=batch, seq=seq, rb=min(seq, cfg['rb']), ch=ch)
    conv_new = h.reshape(batch, seq, N_MAIN)[:, seq - (CONV_W - 1):, 3 * GROUP_W:5 * GROUP_W]

    x = _outproj(x, att, ml, p['w_out'], tm=cfg['tm_out'], tn=cfg['tn_out'])
    x = _mlp(x, row2(p['norm2_w']), p['w_up'], p['w_down'], tm=cfg['tm_mlp'], tf=cfg['tf'])
    state = (k_out.reshape(batch, seq, N_HEADS, HEAD_W), v_out.reshape(batch, seq, N_HEADS, HEAD_W),
             c1, n1, m1[:, :, 0], conv_new)
    return x, state


_PROMPT_CFG = dict(tm_in=1024, tn_in=512, tm_prep=512, tq=256, tk=256, rb=512,
                   tm_out=1024, tn_out=1024, tm_mlp=512, tf=1024)
_SAMPLE_CFG = dict(tm_in=1024, tn_in=512, tm_prep=512, tk_cache=1024, rb=32,
                   tm_out=1024, tn_out=1024, tm_mlp=512, tf=1024)


def kernel(x_prompt, x_sample, cache_k, cache_v, state_C, state_n, state_m, state_conv,
           norm1_w, w_in, conv_w, conv_b, b_gates, q_norm_w, k_norm_w, lambda_qk,
           att_norm_w, ml_norm_w, w_out, norm2_w, w_up, w_down):
    depth = w_in.shape[0]
    pb, ps, _ = x_prompt.shape
    sb, ss, _ = x_sample.shape
    yp = x_prompt.reshape(pb * ps, D_MODEL)
    ys = x_sample.reshape(sb * ss, D_MODEL)
    past_len = cache_k.shape[2]
    cache_k2 = cache_k.reshape(depth * sb * past_len * N_HEADS, HEAD_W)
    cache_v2 = cache_v.reshape(depth * sb * past_len * N_HEADS, HEAD_W)
    sample_cfg = dict(_SAMPLE_CFG, past=past_len)
    sp, sm = [], []
    for l in range(depth):
        p = {
            'norm1_w': norm1_w[l], 'conv_w': conv_w[l], 'conv_b': conv_b[l], 'b_gates': b_gates[l],
            'q_norm_w': q_norm_w[l], 'k_norm_w': k_norm_w[l], 'lambda_qk': lambda_qk[l],
            'att_norm_w': att_norm_w[l], 'ml_norm_w': ml_norm_w[l], 'norm2_w': norm2_w[l],
            'w_main': w_in[l, :, :N_MAIN].astype(BF16),
            'w_gate': jnp.pad(w_in[l, :, N_MAIN:], ((0, 0), (0, LANES - 2 * N_HEADS))).astype(BF16),
            'w_out': w_out[l].astype(BF16), 'w_up': w_up[l].astype(BF16),
            'w_down': w_down[l].astype(BF16),
        }
        yp, st_p = _layer(yp, l, p, None, batch=pb, seq=ps, cfg=_PROMPT_CFG)
        past = (cache_k2, cache_v2, state_C[l], state_n[l], state_m[l], state_conv[l])
        ys, st_s = _layer(ys, l, p, past, batch=sb, seq=ss, cfg=sample_cfg)
        sp.append(st_p)
        sm.append(st_s)
    outs_p = [jnp.stack([s[i] for s in sp]) for i in range(6)]
    outs_s = [jnp.stack([s[i] for s in sm]) for i in range(6)]
    return (yp.reshape(pb, ps, D_MODEL), ys.reshape(sb, ss, D_MODEL), *outs_p, *outs_s)
```

```python
import functools
import math

import jax
import jax.numpy as jnp
from jax import lax
from jax.experimental import pallas as pl
from jax.experimental.pallas import tpu as pltpu

F32 = jnp.float32
BF16 = jnp.bfloat16

D_MODEL = 2048
N_HEADS = 8
HEAD_W = 128
DQK = 64
GROUP_W = N_HEADS * HEAD_W
N_MAIN = 7 * GROUP_W
CONV_W = 4
D_FF = 4 * D_MODEL
EPS = 1e-6
PROMPT_CHUNK = 64
NEG = -1e30

LANES = 128
SUBLANES = 8
MIB = 1024 * 1024


def _cparams(semantics, vmem_mib):
    return pltpu.CompilerParams(dimension_semantics=semantics, vmem_limit_bytes=vmem_mib * MIB)


def _dot(a, b):
    return jnp.dot(a, b, preferred_element_type=F32)


def _dot_nt(a, b):
    return lax.dot_general(a, b, (((1,), (1,)), ((), ())), preferred_element_type=F32)


def _dot_tn(a, b):
    return lax.dot_general(a, b, (((0,), (0,)), ((), ())), preferred_element_type=F32)


def _rms_rows(x, w):
    ms = jnp.mean(x * x, axis=-1, keepdims=True)
    return x * lax.rsqrt(ms + EPS) * w


_NORM_ROWS = 128


def _inproj_kernel(x_ref, nw_ref, w_ref, wg_ref, h_ref, g_ref, xn_ref):
    tm = x_ref.shape[0]

    @pl.when(pl.program_id(1) == 0)
    def _():
        def norm_rows(r, carry):
            rows = pl.ds(pl.multiple_of(r * _NORM_ROWS, _NORM_ROWS), _NORM_ROWS)
            xn_ref[rows, :] = _rms_rows(x_ref[rows, :], nw_ref[...]).astype(BF16)
            return carry

        lax.fori_loop(0, tm // _NORM_ROWS, norm_rows, 0)
        g_ref[...] = _dot(xn_ref[...], wg_ref[...])

    h_ref[...] = _dot(xn_ref[...], w_ref[...])


def _inproj(x, norm_w, w_main, w_gate, *, tm, tn):
    t = x.shape[0]
    return pl.pallas_call(
        _inproj_kernel,
        grid=(t // tm, N_MAIN // tn),
        in_specs=[
            pl.BlockSpec((tm, D_MODEL), lambda i, j: (i, 0)),
            pl.BlockSpec((1, D_MODEL), lambda i, j: (0, 0)),
            pl.BlockSpec((D_MODEL, tn), lambda i, j: (0, j)),
            pl.BlockSpec((D_MODEL, LANES), lambda i, j: (0, 0)),
        ],
        out_specs=[
            pl.BlockSpec((tm, tn), lambda i, j: (i, j)),
            pl.BlockSpec((tm, LANES), lambda i, j: (i, 0)),
        ],
        out_shape=[
            jax.ShapeDtypeStruct((t, N_MAIN), F32),
            jax.ShapeDtypeStruct((t, LANES), F32),
        ],
        scratch_shapes=[pltpu.VMEM((tm, D_MODEL), BF16)],
        compiler_params=_cparams(("parallel", "arbitrary"), 48),
        name="inproj",
    )(x, norm_w, w_main, w_gate)


_PREP_ROWS = 64
_PREP_SLAB = 256


def _group_mean_sq(x, gmat):
    sq = x * x
    hi = sq.astype(BF16)
    lo = (sq - hi.astype(F32)).astype(BF16)
    return _dot(hi, gmat) + _dot(lo, gmat)


def _attn_prep_kernel(q_ref, k_ref, v_ref, qw_ref, kw_ref, gmat_ref,
                      qn_ref, ko_ref, kb_ref, vo_ref, vb_ref):
    tm = q_ref.shape[0]
    gmat = gmat_ref[...]

    def rows_body(r, carry):
        start = pl.multiple_of(r * _PREP_ROWS, _PREP_ROWS)
        rows = pl.ds(start, _PREP_ROWS)
        for s in range(GROUP_W // _PREP_SLAB):
            cols = slice(s * _PREP_SLAB, (s + 1) * _PREP_SLAB)
            q = q_ref[rows, cols]
            qn = q * lax.rsqrt(_group_mean_sq(q, gmat) + EPS) * qw_ref[...]
            qn_ref[rows, cols] = (qn * (DQK ** -0.5)).astype(BF16)
            k = k_ref[rows, cols]
            kn = k * lax.rsqrt(_group_mean_sq(k, gmat) + EPS) * kw_ref[...]
            kb_ref[rows, cols] = kn.astype(BF16)
            v = v_ref[rows, cols]
            vb_ref[rows, cols] = v.astype(BF16)
            for hh in range(_PREP_SLAB // HEAD_W):
                head = s * (_PREP_SLAB // HEAD_W) + hh
                dst = pl.ds(start * N_HEADS + head, _PREP_ROWS, stride=N_HEADS)
                ko_ref[dst, :] = kn[:, hh * HEAD_W:(hh + 1) * HEAD_W]
                vo_ref[dst, :] = v[:, hh * HEAD_W:(hh + 1) * HEAD_W]
        return carry

    lax.fori_loop(0, tm // _PREP_ROWS, rows_body, 0)


def _attn_prep(h, qw, kw, gmat, *, tm):
    t = h.shape[0]
    col = lambda c: pl.BlockSpec((tm, GROUP_W), lambda i, c=c: (i, c))
    small = lambda a: pl.BlockSpec(a.shape, lambda i: (0, 0))
    out = pl.BlockSpec((tm, GROUP_W), lambda i: (i, 0))
    state = pl.BlockSpec((tm * N_HEADS, HEAD_W), lambda i: (i, 0))
    return pl.pallas_call(
        _attn_prep_kernel,
        grid=(t // tm,),
        in_specs=[col(0), col(1), col(2), small(qw), small(kw), small(gmat)],
        out_specs=[out, state, out, state, out],
        out_shape=[
            jax.ShapeDtypeStruct((t, GROUP_W), BF16),
            jax.ShapeDtypeStruct((t * N_HEADS, HEAD_W), F32),
            jax.ShapeDtypeStruct((t, GROUP_W), BF16),
            jax.ShapeDtypeStruct((t * N_HEADS, HEAD_W), F32),
            jax.ShapeDtypeStruct((t, GROUP_W), BF16),
        ],
        compiler_params=_cparams(("parallel",), 48),
        name="attn_prep",
    )(h, h, h, qw, kw, gmat)


def _lambda_full(lq_ref, lam_init):
    lq = lq_ref[...]
    s1 = jnp.sum(lq[0:1] * lq[1:2], axis=-1, keepdims=True)
    s2 = jnp.sum(lq[2:3] * lq[3:4], axis=-1, keepdims=True)
    return jnp.exp(s1) - jnp.exp(s2) + lam_init


def _split_halves(q):
    lane = lax.broadcasted_iota(jnp.int32, q.shape, 1)
    zero = jnp.zeros_like(q)
    return jnp.where(lane < DQK, q, zero), jnp.where(lane >= DQK, q, zero)


def _stack_maps(q):
    return jnp.concatenate(_split_halves(q), axis=0)


def _diff_finish(acc, l, lam, nw, lam_init):
    rows = acc.shape[0] // 2
    a = acc[:rows] / l[:rows] - lam * (acc[rows:] / l[rows:])
    return _rms_rows(a, nw) * (1.0 - lam_init)


_STRIP = 16


def _flash_prompt_kernel(lq_ref, nw_ref, q_ref, k_ref, v_ref, o_ref,
                         s_ref, p_ref, mpart_ref, lpart_ref, *, tq, lam_init):
    seq = q_ref.shape[0]
    lam = _lambda_full(lq_ref, lam_init)
    nw = nw_ref[...]
    rows2 = 2 * tq
    r_i = lax.broadcasted_iota(jnp.int32, (rows2, tq), 0)
    c_i = lax.broadcasted_iota(jnp.int32, (rows2, tq), 1)
    diag_visible = c_i // PROMPT_CHUNK <= (r_i % tq) // PROMPT_CHUNK

    for qi in range(seq // tq):
        lo, vis = qi * tq, (qi + 1) * tq
        n_tiles = vis // LANES
        q2 = _stack_maps(q_ref[lo:vis, :])
        if qi > 0:
            s_ref[:, 0:lo] = _dot_nt(q2, k_ref[0:lo, :])
        s_ref[:, lo:vis] = jnp.where(diag_visible, _dot_nt(q2, k_ref[lo:vis, :]), NEG)

        def max_strip(t, carry):
            rows = slice(t * _STRIP, (t + 1) * _STRIP)
            m = s_ref[rows, 0:LANES]
            for c in range(1, n_tiles):
                m = jnp.maximum(m, s_ref[rows, c * LANES:(c + 1) * LANES])
            mpart_ref[rows, :] = m
            return carry

        for t in range(rows2 // _STRIP):
            max_strip(t, 0)
        row_max = jnp.max(mpart_ref[...], axis=-1, keepdims=True)
        mpart_ref[...] = jnp.broadcast_to(row_max, (rows2, LANES))

        def exp_strip(t, carry):
            rows = slice(t * _STRIP, (t + 1) * _STRIP)
            m = mpart_ref[rows, :]
            l = jnp.zeros((_STRIP, LANES), F32)
            for c in range(n_tiles):
                cols = slice(c * LANES, (c + 1) * LANES)
                p = jnp.exp(s_ref[rows, cols] - m)
                l = l + p
                p_ref[rows, cols] = p.astype(BF16)
            lpart_ref[rows, :] = l
            return carry

        for t in range(rows2 // _STRIP):
            exp_strip(t, 0)
        row_sum = jnp.sum(lpart_ref[...], axis=-1, keepdims=True)
        acc = _dot(p_ref[:, 0:vis], v_ref[0:vis, :])
        o_ref[lo:vis, :] = _diff_finish(acc, row_sum, lam, nw, lam_init).astype(o_ref.dtype)


def _flash_prompt(qn, kb, vb, lam_qk, att_nw, *, batch, seq, tq, lam_init):
    kernel = functools.partial(_flash_prompt_kernel, tq=tq, lam_init=lam_init)
    head_spec = pl.BlockSpec((seq, HEAD_W), lambda b, h: (b, h))
    return pl.pallas_call(
        kernel,
        grid=(batch, N_HEADS),
        in_specs=[
            pl.BlockSpec(lam_qk.shape, lambda b, h: (0, 0)),
            pl.BlockSpec(att_nw.shape, lambda b, h: (0, 0)),
            head_spec, head_spec, head_spec,
        ],
        out_specs=head_spec,
        out_shape=jax.ShapeDtypeStruct((batch * seq, GROUP_W), BF16),
        scratch_shapes=[
            pltpu.VMEM((2 * tq, seq), F32),
            pltpu.VMEM((2 * tq, seq), BF16),
            pltpu.VMEM((2 * tq, LANES), F32),
            pltpu.VMEM((2 * tq, LANES), F32),
        ],
        compiler_params=_cparams(("parallel", "parallel"), 32),
        name="diff_attn_prompt",
    )(lam_qk, att_nw, qn, kb, vb)


def _attn_sample_kernel(lq_ref, nw_ref, q_ref, kn_ref, vn_ref, kc_ref, vc_ref, o_ref,
                        m_ref, l_ref, acc_ref, *, lam_init):
    j = pl.program_id(1)
    n_blocks = m_ref.shape[0] - 1
    rows2 = 2 * q_ref.shape[0]

    def partial(slot, h, k, v):
        cols = slice(h * HEAD_W, (h + 1) * HEAD_W)
        s = _dot_nt(_stack_maps(q_ref[:, cols]), k)
        m = jnp.max(s, axis=-1, keepdims=True)
        p = jnp.exp(s - m)
        m_ref[slot, h] = jnp.broadcast_to(m, (rows2, LANES))
        l_ref[slot, h] = jnp.broadcast_to(jnp.sum(p, axis=-1, keepdims=True), (rows2, LANES))
        acc_ref[slot, h] = _dot(p.astype(BF16), v)

    tk = kc_ref.shape[0] // N_HEADS
    for h in range(N_HEADS):
        head_rows = pl.ds(h, tk, stride=N_HEADS)
        partial(j, h, kc_ref[head_rows, :].astype(BF16), vc_ref[head_rows, :].astype(BF16))

    @pl.when(j == n_blocks - 1)
    def _():
        lam = _lambda_full(lq_ref, lam_init)
        for h in range(N_HEADS):
            cols = slice(h * HEAD_W, (h + 1) * HEAD_W)
            partial(n_blocks, h, kn_ref[:, cols], vn_ref[:, cols])
            m_all = m_ref[0, h]
            for t in range(1, n_blocks + 1):
                m_all = jnp.maximum(m_all, m_ref[t, h])
            l_all = jnp.zeros((rows2, LANES), F32)
            acc = jnp.zeros((rows2, HEAD_W), F32)
            for t in range(n_blocks + 1):
                w = jnp.exp(m_ref[t, h] - m_all)
                l_all = l_all + w * l_ref[t, h]
                acc = acc + w * acc_ref[t, h]
            out = _diff_finish(acc, l_all[:, 0:1], lam, nw_ref[...], lam_init)
            o_ref[:, cols] = out.astype(o_ref.dtype)


def _attn_sample(qn, kb, vb, cache_k, cache_v, lam_qk, att_nw, *, layer, batch, past, seq, tk,
                 lam_init):
    kernel = functools.partial(_attn_sample_kernel, lam_init=lam_init)
    new_spec = pl.BlockSpec((seq, GROUP_W), lambda b, j: (b, 0))
    nkv = past // tk
    cache_spec = pl.BlockSpec((tk * N_HEADS, HEAD_W), lambda b, j: ((layer * batch + b) * nkv + j, 0))
    return pl.pallas_call(
        kernel,
        grid=(batch, nkv),
        in_specs=[
            pl.BlockSpec(lam_qk.shape, lambda b, j: (0, 0)),
            pl.BlockSpec(att_nw.shape, lambda b, j: (0, 0)),
            new_spec, new_spec, new_spec, cache_spec, cache_spec,
        ],
        out_specs=new_spec,
        out_shape=jax.ShapeDtypeStruct((batch * seq, GROUP_W), BF16),
        scratch_shapes=[pltpu.VMEM((nkv + 1, N_HEADS, 2 * seq, LANES), F32) for _ in range(3)],
        compiler_params=_cparams(("parallel", "arbitrary"), 40),
        name="diff_attn_sample",
    )(lam_qk, att_nw, qn, kb, vb, cache_k, cache_v)


def _log_sigmoid(x):
    return jnp.minimum(x, 0.0) - jnp.log(1.0 + jnp.exp(-jnp.abs(x)))


def _cumsum(x, n, axis):
    pos = lax.broadcasted_iota(jnp.int32, x.shape, axis)
    s = 1
    while s < n:
        x = x + jnp.where(pos >= s, pltpu.roll(x, s, axis), 0.0)
        s *= 2
    return x


def _causal_conv_silu(win, w_ref, b_ref, cols, ch):
    acc = b_ref[:, cols] + w_ref[CONV_W - 1:CONV_W, cols] * win[SUBLANES:, :]
    for j in range(CONV_W - 1):
        shifted = pltpu.roll(win, CONV_W - 1 - j, 0)[SUBLANES:, :]
        acc = acc + w_ref[j:j + 1, cols] * shifted
    return acc * jax.nn.sigmoid(acc)


def _mlstm_kernel(xq_ref, xk_ref, xv_ref, xo_ref, hq_ref, hk_ref, pad_ref, gcol_ref, grow_ref,
                  bcol_ref, brow_ref, cw_ref, cb_ref, nw_ref, c0_ref, n0_ref, m0_ref,
                  o_ref, c1_ref, n1_ref, m1_ref,
                  qext_ref, kext_ref, c_ref, n_ref, m_ref, *, ch):
    r = pl.program_id(1)
    rb = xq_ref.shape[0]
    first = r == 0

    @pl.when(first)
    def _():
        c_ref[...] = c0_ref[0]
        n_ref[...] = n0_ref[0]
        m_ref[...] = m0_ref[0]

    qext_ref[0:SUBLANES, :] = jnp.where(first, pad_ref[0, :, 0:GROUP_W], hq_ref[...])
    kext_ref[0:SUBLANES, :] = jnp.where(first, pad_ref[0, :, GROUP_W:2 * GROUP_W], hk_ref[...])
    qext_ref[SUBLANES:, :] = xq_ref[...]
    kext_ref[SUBLANES:, :] = xk_ref[...]

    row_i = lax.broadcasted_iota(jnp.int32, (ch, ch), 0)
    col_i = lax.broadcasted_iota(jnp.int32, (ch, ch), 1)
    causal = col_i <= row_i

    def chunk(c, carry):
        start = pl.multiple_of(c * ch, ch)
        rows = pl.ds(start, ch)
        win = pl.ds(start, ch + SUBLANES)
        qc = _causal_conv_silu(qext_ref[win, :], cw_ref, cb_ref, slice(0, GROUP_W), ch).astype(BF16)
        kc = _causal_conv_silu(kext_ref[win, :], cw_ref, cb_ref, slice(GROUP_W, 2 * GROUP_W), ch)
        kc = (kc * (HEAD_W ** -0.5)).astype(BF16)

        g = gcol_ref[rows, :] + bcol_ref[...]
        b_col = _cumsum(_log_sigmoid(g), ch, 0)
        gr = grow_ref[0, c] + brow_ref[...]
        b_row = _cumsum(_log_sigmoid(gr[N_HEADS:2 * N_HEADS]), ch, 1)
        r_row = gr[0:N_HEADS] - b_row

        for h in range(N_HEADS):
            cols = slice(h * HEAD_W, (h + 1) * HEAD_W)
            q = qc[:, cols]
            k = kc[:, cols]
            v = xv_ref[rows, cols].astype(BF16)
            bc = b_col[:, N_HEADS + h:N_HEADS + h + 1]
            igc = g[:, h:h + 1]
            m_prev = m_ref[h:h + 1, 0:1]
            n_prev = n_ref[h:h + 1, :]
            c_prev = c_ref[h]

            log_d = jnp.where(causal, bc + r_row[h:h + 1, 0:ch], NEG)
            log_inter = bc + m_prev
            m_t = jnp.maximum(log_inter, jnp.max(log_d, axis=-1, keepdims=True))
            d = jnp.exp(log_d - m_t)
            w_inter = jnp.exp(log_inter - m_t)
            qk = _dot_nt(q, k) * d
            num = w_inter * _dot(q, c_prev.astype(BF16)) + _dot(qk.astype(BF16), v)
            qn = jnp.sum(q.astype(F32) * n_prev, axis=-1, keepdims=True)
            den = w_inter * qn + jnp.sum(qk, axis=-1, keepdims=True)
            hv = num / jnp.maximum(jnp.abs(den), jnp.exp(-m_t))

            m_new = m_t[ch - 1:ch, :]
            b_last = bc[ch - 1:ch, :]
            w_c = jnp.exp(b_last + m_prev - m_new)
            w_s = jnp.exp(b_last - bc + igc - m_new)
            kw = k.astype(F32) * w_s
            c_ref[h] = w_c * c_prev + _dot_tn(kw.astype(BF16), v)
            n_ref[h:h + 1, :] = w_c * n_prev + jnp.sum(kw, axis=0, keepdims=True)
            m_ref[h:h + 1, :] = jnp.broadcast_to(m_new, (1, LANES))

            gate = jax.nn.sigmoid(xo_ref[rows, cols])
            o_ref[rows, cols] = (_rms_rows(hv, nw_ref[...]) * gate).astype(o_ref.dtype)
        return carry

    lax.fori_loop(0, rb // ch, chunk, 0)

    @pl.when(r == pl.num_programs(1) - 1)
    def _():
        c1_ref[0] = c_ref[...]
        n1_ref[0] = n_ref[...]
        m1_ref[0] = m_ref[...]


def _mlstm(h, gcol, grow, bcol, brow, conv_pad, conv_w, conv_b, ml_nw, c0, n0, m0,
           *, batch, seq, rb, ch):
    nblk = seq // rb
    kernel = functools.partial(_mlstm_kernel, ch=ch)
    col = lambda c: pl.BlockSpec((rb, GROUP_W), lambda b, r, c=c: (b * nblk + r, c))
    halo = lambda c: pl.BlockSpec(
        (SUBLANES, GROUP_W),
        lambda b, r, c=c: (jnp.maximum((b * seq + r * rb) // SUBLANES - 1, 0), c))
    whole = lambda a: pl.BlockSpec(a.shape, lambda b, r: (0,) * a.ndim)
    per_b = lambda a: pl.BlockSpec((1,) + a.shape[1:], lambda b, r: (b,) + (0,) * (a.ndim - 1))
    return pl.pallas_call(
        kernel,
        grid=(batch, nblk),
        in_specs=[
            col(3), col(4), col(5), col(6), halo(3), halo(4), per_b(conv_pad),
            pl.BlockSpec((rb, LANES), lambda b, r: (b * nblk + r, 0)),
            pl.BlockSpec((1, rb // ch, 2 * N_HEADS, LANES), lambda b, r: (b, r, 0, 0)),
            whole(bcol), whole(brow), whole(conv_w), whole(conv_b), whole(ml_nw),
            per_b(c0), per_b(n0), per_b(m0),
        ],
        out_specs=[
            pl.BlockSpec((rb, GROUP_W), lambda b, r: (b * nblk + r, 0)),
            per_b(c0), per_b(n0), per_b(m0),
        ],
        out_shape=[
            jax.ShapeDtypeStruct((batch * seq, GROUP_W), BF16),
            jax.ShapeDtypeStruct(c0.shape, F32),
            jax.ShapeDtypeStruct(n0.shape, F32),
            jax.ShapeDtypeStruct(m0.shape, F32),
        ],
        scratch_shapes=[
            pltpu.VMEM((rb + SUBLANES, GROUP_W), F32),
            pltpu.VMEM((rb + SUBLANES, GROUP_W), F32),
            pltpu.VMEM((N_HEADS, HEAD_W, HEAD_W), F32),
            pltpu.VMEM((N_HEADS, HEAD_W), F32),
            pltpu.VMEM((N_HEADS, LANES), F32),
        ],
        compiler_params=_cparams(("parallel", "arbitrary"), 48),
        name="mlstm",
    )(h, h, h, h, h, h, conv_pad, gcol, grow, bcol, brow, conv_w, conv_b, ml_nw, c0, n0, m0)


def _outproj_kernel(x_ref, a_ref, m_ref, wa_ref, wm_ref, o_ref):
    o_ref[...] = x_ref[...] + _dot(a_ref[...], wa_ref[...]) + _dot(m_ref[...], wm_ref[...])


def _outproj(x, att, ml, w_out, *, tm, tn):
    t = x.shape[0]
    return pl.pallas_call(
        _outproj_kernel,
        grid=(t // tm, D_MODEL // tn),
        in_specs=[
            pl.BlockSpec((tm, tn), lambda i, j: (i, j)),
            pl.BlockSpec((tm, GROUP_W), lambda i, j: (i, 0)),
            pl.BlockSpec((tm, GROUP_W), lambda i, j: (i, 0)),
            pl.BlockSpec((GROUP_W, tn), lambda i, j: (0, j)),
            pl.BlockSpec((GROUP_W, tn), lambda i, j: (1, j)),
        ],
        out_specs=pl.BlockSpec((tm, tn), lambda i, j: (i, j)),
        out_shape=jax.ShapeDtypeStruct((t, D_MODEL), F32),
        compiler_params=_cparams(("parallel", "parallel"), 48),
        name="outproj",
    )(x, att, ml, w_out, w_out)


def _mlp_kernel(x_ref, nw_ref, wu_ref, wd_ref, o_ref, xn_ref):
    tm = x_ref.shape[0]
    f = pl.program_id(1)

    @pl.when(f == 0)
    def _():
        def norm_rows(r, carry):
            rows = pl.ds(pl.multiple_of(r * _NORM_ROWS, _NORM_ROWS), _NORM_ROWS)
            xn_ref[rows, :] = _rms_rows(x_ref[rows, :], nw_ref[...]).astype(BF16)
            return carry

        lax.fori_loop(0, tm // _NORM_ROWS, norm_rows, 0)

    up = jnp.maximum(_dot(xn_ref[...], wu_ref[...]), 0.0)
    part = _dot((up * up).astype(BF16), wd_ref[...])

    @pl.when(f == 0)
    def _():
        o_ref[...] = x_ref[...] + part

    @pl.when(f != 0)
    def _():
        o_ref[...] += part


def _mlp(x, norm_w, w_up, w_down, *, tm, tf):
    t = x.shape[0]
    return pl.pallas_call(
        _mlp_kernel,
        grid=(t // tm, D_FF // tf),
        in_specs=[
            pl.BlockSpec((tm, D_MODEL), lambda i, f: (i, 0)),
            pl.BlockSpec((1, D_MODEL), lambda i, f: (0, 0)),
            pl.BlockSpec((D_MODEL, tf), lambda i, f: (0, f)),
            pl.BlockSpec((tf, D_MODEL), lambda i, f: (f, 0)),
        ],
        out_specs=pl.BlockSpec((tm, D_MODEL), lambda i, f: (i, 0)),
        out_shape=jax.ShapeDtypeStruct((t, D_MODEL), F32),
        scratch_shapes=[pltpu.VMEM((tm, D_MODEL), BF16)],
        compiler_params=_cparams(("parallel", "arbitrary"), 52),
        name="mlp",
    )(x, norm_w, w_up, w_down)


def _group_matrix():
    g = jnp.arange(_PREP_SLAB) // DQK
    return jnp.where(g[:, None] == g[None, :], 1.0 / DQK, 0.0).astype(BF16)


def _layer(x, l, p, past, *, batch, seq, cfg):
    lam_init = 0.8 - 0.6 * math.exp(-0.3 * l)
    ch = min(seq, PROMPT_CHUNK)
    row2 = lambda a: a.reshape(1, -1)

    h, gates = _inproj(x, row2(p['norm1_w']), p['w_main'], p['w_gate'], tm=cfg['tm_in'], tn=cfg['tn_in'])

    qw = row2(jnp.tile(p['q_norm_w'], _PREP_SLAB // DQK))
    kw = row2(jnp.tile(p['k_norm_w'], _PREP_SLAB // DQK))
    qn, k_out, kb, v_out, vb = _attn_prep(h, qw, kw, _group_matrix(), tm=cfg['tm_prep'])

    att_nw = row2(p['att_norm_w'])
    if past is None:
        att = _flash_prompt(qn, kb, vb, p['lambda_qk'], att_nw, batch=batch, seq=seq,
                            tq=cfg['tq'], lam_init=lam_init)
        conv_buf = jnp.zeros((batch, CONV_W - 1, 2 * GROUP_W), F32)
        c0 = jnp.zeros((batch, N_HEADS, HEAD_W, HEAD_W), F32)
        n0 = jnp.zeros((batch, N_HEADS, HEAD_W), F32)
        m0 = jnp.zeros((batch, N_HEADS), F32)
    else:
        cache_k, cache_v, c0, n0, m0, conv_buf = past
        att = _attn_sample(qn, kb, vb, cache_k, cache_v, p['lambda_qk'], att_nw, layer=l,
                           batch=batch, past=cfg['past'], seq=seq, tk=cfg['tk_cache'],
                           lam_init=lam_init)

    n_chunks = seq // ch
    g16 = gates[:, :2 * N_HEADS].reshape(batch, n_chunks, ch, 2 * N_HEADS)
    grow = jnp.pad(jnp.swapaxes(g16, 2, 3), ((0, 0), (0, 0), (0, 0), (0, LANES - ch)))
    bcol = jnp.pad(p['b_gates'], (0, LANES - 2 * N_HEADS)).reshape(1, LANES)
    brow = p['b_gates'].reshape(2 * N_HEADS, 1)
    conv_pad = jnp.pad(conv_buf, ((0, 0), (SUBLANES - (CONV_W - 1), 0), (0, 0)))
    m0_rep = jnp.broadcast_to(m0[:, :, None], (batch, N_HEADS, LANES))
    ml, c1, n1, m1 = _mlstm(h, gates, grow, bcol, brow, conv_pad, p['conv_w'], row2(p['conv_b']),
                            row2(p['ml_norm_w']), c0, n0, m0_rep,
                            batch=batch, seq=seq, rb=min(seq, cfg['rb']), ch=ch)
    conv_new = h.reshape(batch, seq, N_MAIN)[:, seq - (CONV_W - 1):, 3 * GROUP_W:5 * GROUP_W]

    x = _outproj(x, att, ml, p['w_out'], tm=cfg['tm_out'], tn=cfg['tn_out'])
    x = _mlp(x, row2(p['norm2_w']), p['w_up'], p['w_down'], tm=cfg['tm_mlp'], tf=cfg['tf'])
    state = (k_out.reshape(batch, seq, N_HEADS, HEAD_W), v_out.reshape(batch, seq, N_HEADS, HEAD_W),
             c1, n1, m1[:, :, 0], conv_new)
    return x, state


_PROMPT_CFG = dict(tm_in=1024, tn_in=512, tm_prep=512, tq=256, rb=512,
                   tm_out=1024, tn_out=1024, tm_mlp=512, tf=1024)
_SAMPLE_CFG = dict(tm_in=1024, tn_in=512, tm_prep=512, tk_cache=1024, rb=32,
                   tm_out=1024, tn_out=1024, tm_mlp=512, tf=1024)


def kernel(x_prompt, x_sample, cache_k, cache_v, state_C, state_n, state_m, state_conv,
           norm1_w, w_in, conv_w, conv_b, b_gates, q_norm_w, k_norm_w, lambda_qk,
           att_norm_w, ml_norm_w, w_out, norm2_w, w_up, w_down):
    depth = w_in.shape[0]
    pb, ps, _ = x_prompt.shape
    sb, ss, _ = x_sample.shape
    yp = x_prompt.reshape(pb * ps, D_MODEL)
    ys = x_sample.reshape(sb * ss, D_MODEL)
    past_len = cache_k.shape[2]
    cache_k2 = cache_k.reshape(depth * sb * past_len * N_HEADS, HEAD_W)
    cache_v2 = cache_v.reshape(depth * sb * past_len * N_HEADS, HEAD_W)
    sample_cfg = dict(_SAMPLE_CFG, past=past_len)
    sp, sm = [], []
    for l in range(depth):
        p = {
            'norm1_w': norm1_w[l], 'conv_w': conv_w[l], 'conv_b': conv_b[l], 'b_gates': b_gates[l],
            'q_norm_w': q_norm_w[l], 'k_norm_w': k_norm_w[l], 'lambda_qk': lambda_qk[l],
            'att_norm_w': att_norm_w[l], 'ml_norm_w': ml_norm_w[l], 'norm2_w': norm2_w[l],
            'w_main': w_in[l, :, :N_MAIN].astype(BF16),
            'w_gate': jnp.pad(w_in[l, :, N_MAIN:], ((0, 0), (0, LANES - 2 * N_HEADS))).astype(BF16),
            'w_out': w_out[l].astype(BF16), 'w_up': w_up[l].astype(BF16),
            'w_down': w_down[l].astype(BF16),
        }
        yp, st_p = _layer(yp, l, p, None, batch=pb, seq=ps, cfg=_PROMPT_CFG)
        past = (cache_k2, cache_v2, state_C[l], state_n[l], state_m[l], state_conv[l])
        ys, st_s = _layer(ys, l, p, past, batch=sb, seq=ss, cfg=sample_cfg)
        sp.append(st_p)
        sm.append(st_s)
    outs_p = [jnp.stack([s[i] for s in sp]) for i in range(6)]
    outs_s = [jnp.stack([s[i] for s in sm]) for i in range(6)]
    return (yp.reshape(pb, ps, D_MODEL), ys.reshape(sb, ss, D_MODEL), *outs_p, *outs_s)
```
